```python
import jax
import jax.numpy as jnp
from jax import lax
import numpy as np


D_MODEL = 1024
BATCH = 4
SEQ = 4096
DEPTH = 4

GRID_W = 64
CTX_LEN = 256
HEAD_DIM = 64
A_HEADS = 8
A_KV = 2
R_HEADS = 4
R_DK = 64
R_DV = 128
C_HEADS = 8
C_KV = 2
WINDOW = 128
BLOCK = 128
ROPE_BASE = 10000.0
D_FF = 2816
N_EXPERTS = 8
TOP_K = 2
D_FF_EXPERT = 3584
N_DENSE = (DEPTH + 1) // 2
N_MOE = DEPTH // 2
EPS = 1e-6
NEG = -1e30
COL_SIZES = (A_HEADS * HEAD_DIM, A_KV * HEAD_DIM, A_KV * HEAD_DIM,
             R_HEADS * R_DK, R_HEADS * R_DK, R_HEADS * R_DV, R_HEADS * R_DV,
             C_HEADS * HEAD_DIM, C_KV * HEAD_DIM, C_KV * HEAD_DIM,
             3 * D_MODEL)
IN_COLS = sum(COL_SIZES)

kernel_name = "hybrid_dit_gated_branch_block"


def _rmsnorm(x, g):
    x32 = x.astype(jnp.float32)
    y = x32 * lax.rsqrt(jnp.mean(x32 * x32, axis=-1, keepdims=True) + EPS)
    return (y * g.astype(jnp.float32)).astype(x.dtype)


def _modulate(h, shift, scale):
    return h * (1.0 + scale) + shift


def _heads(t, n):
    return t.reshape(t.shape[0], t.shape[1], n, -1)


def _flip(t):
    return None if t is None else t[:, ::-1]


def _rms_heads(t, g):
    t32 = t.astype(jnp.float32)
    y = t32 * lax.rsqrt(jnp.mean(t32 * t32, axis=-1, keepdims=True) + EPS)
    return (y * g.astype(jnp.float32)).astype(t.dtype)


def _rope_tables(row, col, dim):
    n_freq = dim // 4
    inv = ROPE_BASE ** (-jnp.arange(n_freq, dtype=jnp.float32) / n_freq)
    ang = jnp.concatenate([row[:, None] * inv[None, :], col[:, None] * inv[None, :]], axis=-1)
    return jnp.cos(ang), jnp.sin(ang)


def _rope(t, cos, sin):
    half = t.shape[-1] // 2
    cs = cos[None, :, None, :].astype(t.dtype)
    sn = sin[None, :, None, :].astype(t.dtype)
    t1, t2 = t[..., :half], t[..., half:]
    return jnp.concatenate([t1 * cs - t2 * sn, t1 * sn + t2 * cs], axis=-1)


def _split_cols(p):
    out, off = [], 0
    for n in COL_SIZES:
        out.append(p[..., off:off + n])
        off += n
    return out


def _gqa_attend(q, k, v):
    B, Q, Hq, dh = q.shape
    Hkv = k.shape[2]
    qg = q.reshape(B, Q, Hkv, Hq // Hkv, dh)
    s = jnp.einsum('bqkgd,bskd->bkgqs', qg, k).astype(jnp.float32) * (dh ** -0.5)
    p = jax.nn.softmax(s, axis=-1).astype(v.dtype)
    o = jnp.einsum('bkgqs,bskd->bqkgd', p, v)
    return o.reshape(B, Q, Hq * dh)


def _gqa_blocked(q, k, v):
    B, S, Hq, dh = q.shape
    nb = S // BLOCK
    qb = q.reshape(B, nb, BLOCK, Hq, dh).transpose(1, 0, 2, 3, 4)
    o = lax.map(lambda qblk: _gqa_attend(qblk, k, v), qb)
    return o.transpose(1, 0, 2, 3).reshape(B, S, Hq * dh)


def _window_sink_attend(q, k_lat, v_lat, k_ctx, v_ctx, sink):
    B, S, Hq, dh = q.shape
    Hkv = k_lat.shape[2]
    G = Hq // Hkv
    nb = S // BLOCK
    qb = q.reshape(B, nb, BLOCK, Hkv, G, dh)

    def band(t):
        tp = jnp.pad(t, ((0, 0), (BLOCK, BLOCK), (0, 0), (0, 0))).reshape(B, nb + 2, BLOCK, Hkv, dh)
        return jnp.concatenate([tp[:, :-2], tp[:, 1:-1], tp[:, 2:]], axis=2)

    kb, vb = band(k_lat), band(v_lat)
    scale = dh ** -0.5
    s_loc = jnp.einsum('bnqkgd,bnskd->bnkgqs', qb, kb).astype(jnp.float32) * scale
    blk = jnp.arange(nb)[:, None] * BLOCK
    qpos = blk + jnp.arange(BLOCK)[None, :]
    kpos = blk - BLOCK + jnp.arange(3 * BLOCK)[None, :]
    kp = kpos[:, None, :]
    valid = (jnp.abs(kp - qpos[:, :, None]) <= WINDOW) & (kp >= 0) & (kp < S)
    s_loc = jnp.where(valid[None, :, None, None], s_loc, NEG)
    s_ctx = jnp.einsum('bnqkgd,bskd->bnkgqs', qb, k_ctx).astype(jnp.float32) * scale
    snk = sink.astype(jnp.float32).reshape(1, 1, Hkv, G, 1, 1)
    m = jnp.maximum(jnp.maximum(s_loc.max(-1, keepdims=True), s_ctx.max(-1, keepdims=True)), snk)
    e_loc = jnp.exp(s_loc - m)
    e_ctx = jnp.exp(s_ctx - m)
    denom = e_loc.sum(-1, keepdims=True) + e_ctx.sum(-1, keepdims=True) + jnp.exp(snk - m)
    o = (jnp.einsum('bnkgqs,bnskd->bnqkgd', (e_loc / denom).astype(v_lat.dtype), vb)
         + jnp.einsum('bnkgqs,bskd->bnqkgd', (e_ctx / denom).astype(v_lat.dtype), v_ctx))
    return o.reshape(B, S, Hq * dh)


def _sink_attend(q, k, v, sink):
    B, Q, Hq, dh = q.shape
    Hkv = k.shape[2]
    G = Hq // Hkv
    qg = q.reshape(B, Q, Hkv, G, dh)
    s = jnp.einsum('bqkgd,bskd->bkgqs', qg, k).astype(jnp.float32) * (dh ** -0.5)
    snk = sink.astype(jnp.float32).reshape(1, Hkv, G, 1, 1)
    m = jnp.maximum(s.max(-1, keepdims=True), snk)
    e = jnp.exp(s - m)
    denom = e.sum(-1, keepdims=True) + jnp.exp(snk - m)
    o = jnp.einsum('bkgqs,bskd->bqkgd', (e / denom).astype(v.dtype), v)
    return o.reshape(B, Q, Hq * dh)


def _chunk_retention(q, k, v, log_gamma, state0):
    B, T, H, dk = k.shape
    dv = v.shape[-1]
    n = T // BLOCK
    kc = k.astype(jnp.float32).reshape(B, n, BLOCK, H, dk)
    vc = v.astype(jnp.float32).reshape(B, n, BLOCK, H, dv)
    pos = jnp.arange(BLOCK, dtype=jnp.float32)
    to_end = jnp.exp((BLOCK - 1.0 - pos)[:, None] * log_gamma[None, :])
    kv = jnp.einsum('bnjhd,jh,bnjhe->nbhde', kc, to_end, vc)
    chunk_decay = jnp.exp(BLOCK * log_gamma)[None, :, None, None]

    def step(state, kv_n):
        return state * chunk_decay + kv_n, state

    final, prev = lax.scan(step, state0, kv)
    if q is None:
        return None, final
    qc = q.astype(jnp.float32).reshape(B, n, BLOCK, H, dk)
    rel = pos[:, None] - pos[None, :]
    decay = jnp.where(rel >= 0, jnp.exp(jnp.maximum(rel, 0.0)[None] * log_gamma[:, None, None]), 0.0)
    scores = jnp.einsum('bnihd,bnjhd->bnhij', qc, kc) * decay
    intra = jnp.einsum('bnhij,bnjhe->bnihe', scores, vc)
    from_start = jnp.exp((pos + 1.0)[:, None] * log_gamma[None, :])
    inter = jnp.einsum('bnihd,ih,nbhde->bnihe', qc, from_start, prev)
    return (intra + inter).reshape(B, T, H, dv), final


def _retention_out(o, g, gain):
    B, T, H, dv = o.shape
    mu = jnp.mean(o, axis=-1, keepdims=True)
    var = jnp.mean((o - mu) ** 2, axis=-1, keepdims=True)
    y = ((o - mu) * lax.rsqrt(var + EPS)).reshape(B, T, H * dv) * gain.astype(jnp.float32)
    return jax.nn.silu(g) * y.astype(g.dtype)


def _merge(ya, yr, yc, gl, b_gate, w_br_a, w_br_b, w_br_c, w_out):
    ga, gb, gc = jnp.split(jax.nn.sigmoid(gl + b_gate), 3, axis=-1)
    m = ga * (ya @ w_br_a) + gb * (yr @ w_br_b) + gc * (yc @ w_br_c)
    return m @ w_out


def _token_mixer(h_lat, h_ctx, w_in, b_gate, qn_a, kn_a, ret_a, ret_g, sink_c,
                 w_br_a, w_br_b, w_br_c, w_out, cos, sin, need_ctx):
    B = h_lat.shape[0]
    qa, ka, va, qr, kr, vr, gr, qc, kc, vc, gl = _split_cols(h_lat @ w_in)
    qa_x, ka_x, va_x, qr_x, kr_x, vr_x, gr_x, qc_x, kc_x, vc_x, gl_x = _split_cols(h_ctx @ w_in)

    qa = _rope(_rms_heads(_heads(qa, A_HEADS), qn_a), cos, sin)
    ka = _rope(_rms_heads(_heads(ka, A_KV), kn_a), cos, sin)
    ka_x = _rms_heads(_heads(ka_x, A_KV), kn_a)
    va, va_x = _heads(va, A_KV), _heads(va_x, A_KV)
    ya = _gqa_blocked(qa, jnp.concatenate([ka_x, ka], axis=1), jnp.concatenate([va_x, va], axis=1))

    lg = -jax.nn.softplus(-ret_a.astype(jnp.float32))
    rs = R_DK ** -0.5
    qr = _rope(_heads(qr, R_HEADS) * rs, cos, sin)
    kr = _rope(_heads(kr, R_HEADS), cos, sin)
    vr = _heads(vr, R_HEADS)
    kr_x, vr_x = _heads(kr_x, R_HEADS), _heads(vr_x, R_HEADS)
    qr_x = _heads(qr_x, R_HEADS) * rs if need_ctx else None
    zero = jnp.zeros((B, R_HEADS, R_DK, R_DV), jnp.float32)
    or_xf, st_f = _chunk_retention(qr_x, kr_x, vr_x, lg[0], zero)
    or_xb, st_b = _chunk_retention(_flip(qr_x), _flip(kr_x), _flip(vr_x), lg[1], zero)
    or_f, _ = _chunk_retention(qr, kr, vr, lg[0], st_f)
    or_b, _ = _chunk_retention(_flip(qr), _flip(kr), _flip(vr), lg[1], st_b)
    yr = _retention_out(or_f + _flip(or_b), gr, ret_g)

    qc = _rope(_heads(qc, C_HEADS), cos, sin)
    kc = _rope(_heads(kc, C_KV), cos, sin)
    vc = _heads(vc, C_KV)
    kc_x, vc_x = _heads(kc_x, C_KV), _heads(vc_x, C_KV)
    yc = _window_sink_attend(qc, kc, vc, kc_x, vc_x, sink_c)

    out_lat = _merge(ya, yr, yc, gl, b_gate, w_br_a, w_br_b, w_br_c, w_out)
    if not need_ctx:
        return out_lat, None

    ya_x = _gqa_attend(_rms_heads(_heads(qa_x, A_HEADS), qn_a), ka_x, va_x)
    yr_x = _retention_out(or_xf + _flip(or_xb), gr_x, ret_g)
    yc_x = _sink_attend(_heads(qc_x, C_HEADS), kc_x, vc_x, sink_c)
    out_ctx = _merge(ya_x, yr_x, yc_x, gl_x, b_gate, w_br_a, w_br_b, w_br_c, w_out)
    return out_lat, out_ctx


def _swiglu(h, w1, w3, w2):
    return (jax.nn.silu(h @ w1) * (h @ w3)) @ w2


def _moe_swiglu(h, router_w, router_b, w1, w3, w2):
    B, T, D = h.shape
    t = h.reshape(B * T, D)
    logits = (t @ router_w).astype(jnp.float32) + router_b.astype(jnp.float32)
    top_v, top_i = lax.top_k(logits, TOP_K)
    wts = jax.nn.softmax(top_v, axis=-1)
    combine = jnp.sum(jax.nn.one_hot(top_i, N_EXPERTS, dtype=jnp.float32) * wts[..., None], axis=1)
    combine = combine.astype(t.dtype)
    out = jnp.zeros_like(t)
    for e in range(N_EXPERTS):
        out = out + combine[:, e:e + 1] * _swiglu(t, w1[e], w3[e], w2[e])
    return out.reshape(B, T, D)


def setup_inputs(seed: int = 0) -> dict:
    key = jax.random.key(seed)
    ks = jax.random.split(key, 28)
    f32 = jnp.float32

    def nrm(k, shape, scale):
        return jax.random.normal(k, shape, f32) * scale

    def gain(k, shape):
        return 1.0 + 0.02 * jax.random.normal(k, shape, f32)

    base = jnp.log(2.0 ** (5.0 + jnp.arange(R_HEADS, dtype=f32)) - 1.0)
    ret_a = base[None, None, :] + 0.1 * jax.random.normal(ks[13], (DEPTH, 2, R_HEADS), f32)
    D = D_MODEL
    return {
        "x": nrm(ks[0], (BATCH, SEQ, D), 1.0),
        "c": nrm(ks[1], (BATCH, D), 1.0),
        "ctx": nrm(ks[2], (BATCH, CTX_LEN, D), 1.0),
        "c_ctx": nrm(ks[3], (D,), 1.0),
        "w_ada": nrm(ks[4], (DEPTH, D, 6 * D), 0.5 * D ** -0.5),
        "b_ada": nrm(ks[5], (DEPTH, 6 * D), 0.02),
        "norm1_g": gain(ks[6], (DEPTH, D)),
        "norm2_g": gain(ks[7], (DEPTH, D)),
        "final_g": gain(ks[8], (D,)),
        "w_in": nrm(ks[9], (DEPTH, D, IN_COLS), D ** -0.5),
        "b_gate": nrm(ks[10], (DEPTH, 3 * D), 0.02),
        "qn_a": gain(ks[11], (DEPTH, HEAD_DIM)),
        "kn_a": gain(ks[12], (DEPTH, HEAD_DIM)),
        "ret_a": ret_a,
        "ret_g": gain(ks[14], (DEPTH, R_HEADS * R_DV)),
        "sink_c": nrm(ks[15], (DEPTH, C_HEADS), 0.5),
        "w_br_a": nrm(ks[16], (DEPTH, A_HEADS * HEAD_DIM, D), (A_HEADS * HEAD_DIM) ** -0.5),
        "w_br_b": nrm(ks[17], (DEPTH, R_HEADS * R_DV, D), (R_HEADS * R_DV) ** -0.5),
        "w_br_c": nrm(ks[18], (DEPTH, C_HEADS * HEAD_DIM, D), (C_HEADS * HEAD_DIM) ** -0.5),
        "w_out": nrm(ks[19], (DEPTH, D, D), D ** -0.5),
        "ffn_w1": nrm(ks[20], (N_DENSE, D, D_FF), D ** -0.5),
        "ffn_w3": nrm(ks[21], (N_DENSE, D, D_FF), D ** -0.5),
        "ffn_w2": nrm(ks[22], (N_DENSE, D_FF, D), D_FF ** -0.5),
        "router_w": nrm(ks[23], (N_MOE, D, N_EXPERTS), D ** -0.5),
        "router_b": nrm(ks[24], (N_MOE, N_EXPERTS), 0.01),
        "exp_w1": nrm(ks[25], (N_MOE, N_EXPERTS, D, D_FF_EXPERT), D ** -0.5),
        "exp_w3": nrm(ks[26], (N_MOE, N_EXPERTS, D, D_FF_EXPERT), D ** -0.5),
        "exp_w2": nrm(ks[27], (N_MOE, N_EXPERTS, D_FF_EXPERT, D), D_FF_EXPERT ** -0.5),
    }


def reference(x, c, ctx, c_ctx, w_ada, b_ada, norm1_g, norm2_g, final_g, w_in, b_gate,
              qn_a, kn_a, ret_a, ret_g, sink_c, w_br_a, w_br_b, w_br_c, w_out,
              ffn_w1, ffn_w3, ffn_w2, router_w, router_b, exp_w1, exp_w3, exp_w2):
    S = x.shape[1]
    L = ctx.shape[1]
    rows = S // GRID_W
    row = jnp.repeat(jnp.arange(rows), GRID_W).astype(jnp.float32)
    col = jnp.tile(jnp.arange(GRID_W), rows).astype(jnp.float32)
    cos, sin = _rope_tables(row, col, HEAD_DIM)
    sc = jax.nn.silu(c)
    scx = jax.nn.silu(c_ctx)
    xl, xc = x, ctx
    for l in range(DEPTH):
        need_ctx = l < DEPTH - 1
        mod_l = jnp.split((sc @ w_ada[l] + b_ada[l])[:, None, :], 6, axis=-1)
        mod_c = jnp.split(scx @ w_ada[l] + b_ada[l], 6, axis=-1)
        hl = _modulate(_rmsnorm(xl, norm1_g[l]), mod_l[0], mod_l[1])
        hc = _modulate(_rmsnorm(xc, norm1_g[l]), mod_c[0], mod_c[1])
        ol, oc = _token_mixer(hl, hc, w_in[l], b_gate[l], qn_a[l], kn_a[l], ret_a[l], ret_g[l],
                              sink_c[l], w_br_a[l], w_br_b[l], w_br_c[l], w_out[l], cos, sin, need_ctx)
        xl = xl + mod_l[2] * ol
        hl2 = _modulate(_rmsnorm(xl, norm2_g[l]), mod_l[3], mod_l[4])
        if need_ctx:
            xc = xc + mod_c[2] * oc
            hc2 = _modulate(_rmsnorm(xc, norm2_g[l]), mod_c[3], mod_c[4])
            h2 = jnp.concatenate([hc2, hl2], axis=1)
        else:
            h2 = hl2
        j = l // 2
        if l % 2 == 0:
            f = _swiglu(h2, ffn_w1[j], ffn_w3[j], ffn_w2[j])
        else:
            f = _moe_swiglu(h2, router_w[j], router_b[j], exp_w1[j], exp_w3[j], exp_w2[j])
        xl = xl + mod_l[5] * f[:, f.shape[1] - S:]
        if need_ctx:
            xc = xc + mod_c[5] * f[:, :L]
    return _rmsnorm(xl, final_g)
```

```python
import functools

import jax
import jax.numpy as jnp
from jax import lax
from jax.experimental import pallas as pl
from jax.experimental.pallas import tpu as pltpu

F32 = jnp.float32
BF16 = jnp.bfloat16

GRID_W = 64
HEAD_DIM = 64
A_HEADS = 8
R_HEADS = 4
R_DK = 64
R_DV = 128
C_HEADS = 8
WINDOW = 128
ROPE_BASE = 10000.0
N_EXPERTS = 8
TOP_K = 2
EPS = 1e-6
NEG = -1e30

LANES = 128
TOK = 256
MOE_TM = 512
MOE_TF = 1792
VMEM_LIMIT = 56 * 1024 * 1024

QA, KA, VA = (0, 512), (512, 640), (640, 768)
QR, KR, VR, GR = (768, 1024), (1024, 1280), (1280, 1792), (1792, 2304)
QC, KC, VC = (2304, 2816), (2816, 2944), (2944, 3072)
N_QKV = 3072
Q_SCALE = HEAD_DIM ** -0.5


def _cparams(*sem):
    return pltpu.CompilerParams(dimension_semantics=sem, vmem_limit_bytes=VMEM_LIMIT)


def _sigmoid(x):
    return 1.0 / (1.0 + jnp.exp(-x))


def _silu(x):
    return x * _sigmoid(x)


def _norm_mod(x, g, shift, scale):
    ms = jnp.mean(x * x, axis=-1, keepdims=True)
    y = x * lax.rsqrt(ms + EPS)
    y = y * g
    return y * (1.0 + scale) + shift


def _nt_dot(a, b):
    return lax.dot_general(a, b, (((1,), (1,)), ((), ())), preferred_element_type=F32)


def _dot(a, b):
    return jnp.dot(a, b, preferred_element_type=F32)


def _ada_kernel(c_ref, w_ref, b_ref, o_ref):
    sc = _silu(c_ref[...])
    o_ref[0] = _dot(sc.astype(BF16), w_ref[0].astype(BF16)) + b_ref[0]


def _ada_mods(c8, w_ada, b_ada):
    depth, d, n = w_ada.shape
    tn = n // 4
    return pl.pallas_call(
        _ada_kernel,
        out_shape=jax.ShapeDtypeStruct((depth, c8.shape[0], n), F32),
        grid=(depth, n // tn),
        in_specs=[
            pl.BlockSpec(c8.shape, lambda l, j: (0, 0)),
            pl.BlockSpec((1, d, tn), lambda l, j: (l, 0, j)),
            pl.BlockSpec((1, 1, tn), lambda l, j: (l, 0, j)),
        ],
        out_specs=pl.BlockSpec((1, c8.shape[0], tn), lambda l, j: (l, 0, j)),
        compiler_params=_cparams("parallel", "parallel"),
        name="ada_mods",
    )(c8, w_ada, b_ada.reshape(depth, 1, n))


def _proj_kernel(x_ref, mod_ref, g_ref, w_ref, cos_ref, sin_ref, p_ref, qn_ref, kn_ref,
                 qa_ref, ka_ref, va_ref, qr_ref, kr_ref, vr_ref, gr_ref, qc_ref, kc_ref, vc_ref):
    h = _norm_mod(x_ref[0], g_ref[...], mod_ref[0, 0, 0:1, :], mod_ref[0, 0, 1:2, :]).astype(BF16)
    cos = cos_ref[...]
    sin = sin_ref[...]
    lane = lax.broadcasted_iota(jnp.int32, cos.shape, 1)
    first_half = (lane & 32) == 0

    def seg(cols):
        return _dot(h, w_ref[:, cols[0]:cols[1]])

    def head_rsqrt(t):
        w = t.shape[1]
        sq = t * t
        hi = sq.astype(BF16)
        lo = (sq - hi.astype(F32)).astype(BF16)
        p = p_ref[0:w, 0:w]
        return lax.rsqrt(_dot(hi, p) + _dot(lo, p) + EPS)

    def rope_store(t, o_ref, scale, r=None, gain=None):
        for k in range(t.shape[1] // LANES):
            sl = slice(k * LANES, (k + 1) * LANES)
            y = t[:, sl]
            if r is not None:
                y = y * r[:, sl]
                y = y * gain
            if scale != 1.0:
                y = y * scale
            swapped = jnp.where(first_half, pltpu.roll(y, LANES - 32, 1), pltpu.roll(y, 32, 1))
            o_ref[0, :, sl] = (y * cos + swapped * sin).astype(o_ref.dtype)

    t = seg(QA)
    rope_store(t, qa_ref, Q_SCALE, head_rsqrt(t), qn_ref[...])
    t = seg((KA[0], VA[1]))
    k = t[:, 0:LANES]
    rope_store(k, ka_ref, 1.0, head_rsqrt(k), kn_ref[...])
    va_ref[0] = t[:, LANES:].astype(va_ref.dtype)
    t = seg((QR[0], KR[1]))
    rope_store(t[:, 0:256], qr_ref, R_DK ** -0.5)
    rope_store(t[:, 256:512], kr_ref, 1.0)
    vr_ref[0] = seg(VR)
    gr_ref[0] = seg(GR)
    rope_store(seg(QC), qc_ref, Q_SCALE)
    t = seg((KC[0], VC[1]))
    rope_store(t[:, 0:LANES], kc_ref, 1.0)
    vc_ref[0] = t[:, LANES:].astype(vc_ref.dtype)


def _proj(x, mods, g1, w_qkv, cos, sin, pmat, qn, kn, n_ctx_blocks):
    b, t, d = x.shape
    nblk = t // TOK
    widths = [(512, BF16), (128, BF16), (128, BF16), (256, F32), (256, F32), (512, F32), (512, F32),
              (512, BF16), (128, BF16), (128, BF16)]
    tok_map = lambda bi, i: (bi, i, 0)
    const2 = lambda bi, i: (0, 0)
    return pl.pallas_call(
        _proj_kernel,
        out_shape=[jax.ShapeDtypeStruct((b, t, w), dt) for w, dt in widths],
        grid=(b, nblk),
        in_specs=[
            pl.BlockSpec((1, TOK, d), tok_map),
            pl.BlockSpec((1, 1, 8, d), lambda bi, i: (bi, jnp.where(i >= n_ctx_blocks, 1, 0), 0, 0)),
            pl.BlockSpec((1, d), const2),
            pl.BlockSpec(w_qkv.shape, const2),
            pl.BlockSpec((TOK, LANES), lambda bi, i: (i, 0)),
            pl.BlockSpec((TOK, LANES), lambda bi, i: (i, 0)),
            pl.BlockSpec(pmat.shape, const2),
            pl.BlockSpec((1, LANES), const2),
            pl.BlockSpec((1, LANES), const2),
        ],
        out_specs=[pl.BlockSpec((1, TOK, w), tok_map) for w, _ in widths],
        compiler_params=_cparams("parallel", "parallel"),
        name="proj_qkv",
    )(x, mods, g1, w_qkv, cos, sin, pmat, qn, kn)


def _pair_queries(q_ref, in_group0):
    q = q_ref[0].astype(F32)
    lane = lax.broadcasted_iota(jnp.int32, q.shape, 1)
    left = lane < HEAD_DIM
    qa = jnp.where(left, q, 0.0)
    qb = jnp.where(left, 0.0, q)
    q0 = jnp.where(in_group0, qa, pltpu.roll(qa, HEAD_DIM, 1)).astype(BF16)
    q1 = jnp.where(in_group0, pltpu.roll(qb, HEAD_DIM, 1), qb).astype(BF16)
    return q0, q1, left


def _pair_store(o_ref, o0, o1, in_group0, left):
    a = jnp.where(in_group0, o0, pltpu.roll(o0, HEAD_DIM, 1))
    b = jnp.where(in_group0, pltpu.roll(o1, HEAD_DIM, 1), o1)
    o_ref[0] = jnp.where(left, a, b).astype(o_ref.dtype)


def _attn_a_kernel(q_ref, k_ref, v_ref, o_ref, *, n_ctx_blocks, n_ctx, blk0):
    i = pl.program_id(1) + blk0
    in_group0 = pl.program_id(2) < 2
    q0, q1, left = _pair_queries(q_ref, in_group0)

    def attend(nk):
        k = k_ref[0, 0:nk, :]
        v = v_ref[0, 0:nk, :]
        outs = []
        for q in (q0, q1):
            s = _nt_dot(q, k)
            m = jnp.max(s, axis=-1, keepdims=True)
            p = jnp.exp(s - m)
            l = jnp.sum(p, axis=-1, keepdims=True)
            outs.append(_dot(p.astype(BF16), v) / l)
        _pair_store(o_ref, outs[0], outs[1], in_group0, left)

    @pl.when(i < n_ctx_blocks)
    def _():
        attend(n_ctx)

    @pl.when(i >= n_ctx_blocks)
    def _():
        attend(k_ref.shape[1])


def _attn_a(q, k, v, n_ctx, blk0):
    b, t, _ = q.shape
    nblk = t // TOK
    kern = functools.partial(_attn_a_kernel, n_ctx_blocks=n_ctx // TOK, n_ctx=n_ctx, blk0=blk0)
    return pl.pallas_call(
        kern,
        out_shape=jax.ShapeDtypeStruct(q.shape, BF16),
        grid=(b, nblk - blk0, A_HEADS // 2),
        in_specs=[
            pl.BlockSpec((1, TOK, LANES), lambda bi, i, j: (bi, i + blk0, j)),
            pl.BlockSpec((1, t, LANES), lambda bi, i, j: (bi, 0, 0)),
            pl.BlockSpec((1, t, LANES), lambda bi, i, j: (bi, 0, 0)),
        ],
        out_specs=pl.BlockSpec((1, TOK, LANES), lambda bi, i, j: (bi, i + blk0, j)),
        compiler_params=_cparams("parallel", "parallel", "parallel"),
        name="attn_global",
    )(q, k, v)


def _attn_c_kernel(sink_ref, q_ref, k_ref, v_ref, o_ref, *, n_ctx_blocks, n_ctx, blk0):
    i = pl.program_id(1) + blk0
    j = pl.program_id(2)
    in_group0 = j < 2
    q0, q1, left = _pair_queries(q_ref, in_group0)
    t = k_ref.shape[1]
    wk = TOK + 2 * WINDOW
    tok0 = i * TOK
    start = pl.multiple_of(jnp.clip(tok0 - WINDOW, n_ctx, t - wk), LANES)
    k_ctx = k_ref[0, 0:n_ctx, :]
    v_ctx = v_ref[0, 0:n_ctx, :]
    k_win = k_ref[0, pl.ds(start, wk), :]
    v_win = v_ref[0, pl.ds(start, wk), :]
    qpos = tok0 + lax.broadcasted_iota(jnp.int32, (TOK, wk), 0)
    kpos = start + lax.broadcasted_iota(jnp.int32, (TOK, wk), 1)
    valid = (jnp.abs(kpos - qpos) <= WINDOW) & (i >= n_ctx_blocks)
    outs = []
    for hh, q in enumerate((q0, q1)):
        sink = sink_ref[2 * j + hh]
        s_ctx = _nt_dot(q, k_ctx)
        s_loc = jnp.where(valid, _nt_dot(q, k_win), NEG)
        m = jnp.maximum(jnp.maximum(jnp.max(s_loc, axis=-1, keepdims=True),
                                    jnp.max(s_ctx, axis=-1, keepdims=True)), sink)
        e_loc = jnp.exp(s_loc - m)
        e_ctx = jnp.exp(s_ctx - m)
        denom = (jnp.sum(e_loc, axis=-1, keepdims=True) + jnp.sum(e_ctx, axis=-1, keepdims=True)
                 + jnp.exp(sink - m))
        outs.append(_dot((e_loc / denom).astype(BF16), v_win) + _dot((e_ctx / denom).astype(BF16), v_ctx))
    _pair_store(o_ref, outs[0], outs[1], in_group0, left)


def _attn_c(sink, q, k, v, n_ctx, blk0):
    b, t, _ = q.shape
    nblk = t // TOK
    kern = functools.partial(_attn_c_kernel, n_ctx_blocks=n_ctx // TOK, n_ctx=n_ctx, blk0=blk0)
    return pl.pallas_call(
        kern,
        out_shape=jax.ShapeDtypeStruct(q.shape, BF16),
        grid=(b, nblk - blk0, C_HEADS // 2),
        in_specs=[
            pl.BlockSpec(memory_space=pltpu.SMEM),
            pl.BlockSpec((1, TOK, LANES), lambda bi, i, j: (bi, i + blk0, j)),
            pl.BlockSpec((1, t, LANES), lambda bi, i, j: (bi, 0, 0)),
            pl.BlockSpec((1, t, LANES), lambda bi, i, j: (bi, 0, 0)),
        ],
        out_specs=pl.BlockSpec((1, TOK, LANES), lambda bi, i, j: (bi, i + blk0, j)),
        compiler_params=_cparams("parallel", "parallel", "parallel"),
        name="attn_window",
    )(sink, q, k, v)


def _ret_kernel(lg_ref, qf_ref, kf_ref, vf_ref, qb_ref, kb_ref, vb_ref, of_ref, ob_ref, sf_ref, sb_ref):
    c = qf_ref.shape[1]

    @pl.when(pl.program_id(1) == 0)
    def _():
        sf_ref[...] = jnp.zeros_like(sf_ref)
        sb_ref[...] = jnp.zeros_like(sb_ref)

    ii = lax.broadcasted_iota(jnp.int32, (c, c), 0)
    jj = lax.broadcasted_iota(jnp.int32, (c, c), 1)
    pos = lax.broadcasted_iota(jnp.int32, (c, 1), 0).astype(F32)
    lane = lax.broadcasted_iota(jnp.int32, (c, LANES), 1)
    for direction, (q_ref, k_ref, v_ref, o_ref, s_ref) in enumerate(
            ((qf_ref, kf_ref, vf_ref, of_ref, sf_ref), (qb_ref, kb_ref, vb_ref, ob_ref, sb_ref))):
        if direction == 0:
            rel = (ii - jj).astype(F32)
            n_from_start = pos + 1.0
            n_to_end = (c - 1.0) - pos
        else:
            rel = (jj - ii).astype(F32)
            n_from_start = c - pos
            n_to_end = pos
        for h in range(R_HEADS):
            lg = lg_ref[direction, h]
            pair = slice((h // 2) * LANES, (h // 2 + 1) * LANES)
            own = (lane < R_DK) if h % 2 == 0 else (lane >= R_DK)
            qm = jnp.where(own, q_ref[0, :, pair], 0.0)
            km = jnp.where(own, k_ref[0, :, pair], 0.0)
            vsl = slice(h * R_DV, (h + 1) * R_DV)
            v = v_ref[0, :, vsl].astype(BF16)
            decay = jnp.where(rel >= 0.0, jnp.exp(jnp.maximum(rel, 0.0) * lg), 0.0)
            scores = _nt_dot(qm.astype(BF16), km.astype(BF16)) * decay
            intra = _dot(scores.astype(BF16), v)
            state = s_ref[h]
            inter = _dot((qm * jnp.exp(n_from_start * lg)).astype(BF16), state.astype(BF16))
            o_ref[0, :, vsl] = intra + inter
            kd = (km * jnp.exp(n_to_end * lg)).astype(BF16)
            kv = lax.dot_general(kd, v, (((0,), (0,)), ((), ())), preferred_element_type=F32)
            chunk_decay = jnp.exp(jnp.full((1, LANES), float(c), F32) * lg)
            s_ref[h] = state * chunk_decay + kv


def _retention(lg, q, k, v, n_ctx):
    b, t, _ = q.shape
    nblk = t // TOK
    nc = n_ctx // TOK

    def fwd(bi, n):
        return (bi, n, 0)

    def bwd(bi, n):
        return (bi, jnp.where(n < nc, nc - 1 - n, nblk - 1 - (n - nc)), 0)

    qk_spec = lambda m: pl.BlockSpec((1, TOK, q.shape[2]), m)
    v_spec = lambda m: pl.BlockSpec((1, TOK, v.shape[2]), m)
    return pl.pallas_call(
        _ret_kernel,
        out_shape=[jax.ShapeDtypeStruct(v.shape, F32)] * 2,
        grid=(b, nblk),
        in_specs=[pl.BlockSpec(memory_space=pltpu.SMEM),
                  qk_spec(fwd), qk_spec(fwd), v_spec(fwd), qk_spec(bwd), qk_spec(bwd), v_spec(bwd)],
        out_specs=[v_spec(fwd), v_spec(bwd)],
        scratch_shapes=[pltpu.VMEM((R_HEADS, LANES, R_DV), F32)] * 2,
        compiler_params=_cparams("parallel", "arbitrary"),
        name="retention",
    )(lg, q, k, v, q, k, v)


def _merge_kernel(x_ref, mod_ref, g_ref, wg_ref, bg_ref, ya_ref, of_ref, ob_ref, gr_ref, rg_ref, yc_ref,
                  wa_ref, wb_ref, wc_ref, wo_ref, o_ref):
    x = x_ref[0]
    d = x.shape[1]
    h = _norm_mod(x, g_ref[...], mod_ref[0, 0, 0:1, :], mod_ref[0, 0, 1:2, :]).astype(BF16)
    yr = []
    for hh in range(R_HEADS):
        sl = slice(hh * R_DV, (hh + 1) * R_DV)
        o = of_ref[0, :, sl] + ob_ref[0, :, sl]
        mu = jnp.mean(o, axis=-1, keepdims=True)
        dev = o - mu
        var = jnp.mean(dev * dev, axis=-1, keepdims=True)
        y = (dev * lax.rsqrt(var + EPS)) * rg_ref[:, sl]
        yr.append((_silu(gr_ref[0, :, sl]) * y).astype(BF16))
    yr = jnp.concatenate(yr, axis=1)
    branches = ((ya_ref[0], wa_ref), (yr, wb_ref), (yc_ref[0], wc_ref))
    m = None
    for n, (y, w_ref) in enumerate(branches):
        gl = _dot(h, wg_ref[:, n * d:(n + 1) * d]) + bg_ref[:, n * d:(n + 1) * d]
        term = _sigmoid(gl) * _dot(y, w_ref[...])
        m = term if m is None else m + term
    out = _dot(m.astype(BF16), wo_ref[...])
    o_ref[0] = x + mod_ref[0, 0, 2:3, :] * out


def _merge(x, mods, g1, w_gate, b_gate, ya, o_f, o_b, gr, ret_g, yc, wa, wb, wc, wo, n_ctx_blocks, blk0):
    b, t, d = x.shape
    nblk = t // TOK
    tok = lambda w: pl.BlockSpec((1, TOK, w), lambda bi, i: (bi, i + blk0, 0))
    const = lambda a: pl.BlockSpec(a.shape, lambda bi, i: (0, 0))
    return pl.pallas_call(
        _merge_kernel,
        out_shape=jax.ShapeDtypeStruct(x.shape, F32),
        grid=(b, nblk - blk0),
        in_specs=[
            tok(d),
            pl.BlockSpec((1, 1, 8, d), lambda bi, i: (bi, jnp.where(i + blk0 >= n_ctx_blocks, 1, 0), 0, 0)),
            const(g1), const(w_gate), const(b_gate),
            tok(512), tok(512), tok(512), tok(512), const(ret_g), tok(512),
            const(wa), const(wb), const(wc), const(wo),
        ],
        out_specs=tok(d),
        compiler_params=_cparams("parallel", "parallel"),
        name="merge",
    )(x, mods, g1, w_gate, b_gate, ya, o_f, o_b, gr, ret_g, yc, wa, wb, wc, wo)


def _ffn_kernel(x_ref, mod_ref, g_ref, w1_ref, w3_ref, w2_ref, o_ref):
    x = x_ref[0]
    h = _norm_mod(x, g_ref[...], mod_ref[0, 0, 3:4, :], mod_ref[0, 0, 4:5, :]).astype(BF16)
    u = (_silu(_dot(h, w1_ref[...])) * _dot(h, w3_ref[...])).astype(BF16)
    o_ref[0] = x + mod_ref[0, 0, 5:6, :] * _dot(u, w2_ref[...])


def _ffn(x, mods, g2, w1, w3, w2, n_ctx_blocks, blk0):
    b, t, d = x.shape
    nblk = t // TOK
    tok = pl.BlockSpec((1, TOK, d), lambda bi, i: (bi, i + blk0, 0))
    const = lambda a: pl.BlockSpec(a.shape, lambda bi, i: (0, 0))
    return pl.pallas_call(
        _ffn_kernel,
        out_shape=jax.ShapeDtypeStruct(x.shape, F32),
        grid=(b, nblk - blk0),
        in_specs=[
            tok,
            pl.BlockSpec((1, 1, 8, d), lambda bi, i: (bi, jnp.where(i + blk0 >= n_ctx_blocks, 1, 0), 0, 0)),
            const(g2), const(w1), const(w3), const(w2),
        ],
        out_specs=tok,
        compiler_params=_cparams("parallel", "parallel"),
        name="ffn_dense",
    )(x, mods, g2, w1, w3, w2)


def _router_kernel(x_ref, mod_ref, g_ref, wr_ref, br_ref, h_ref, lg_ref):
    h = _norm_mod(x_ref[0], g_ref[...], mod_ref[0, 0, 3:4, :], mod_ref[0, 0, 4:5, :])
    h_ref[0] = h.astype(h_ref.dtype)
    lg_ref[0] = jnp.dot(h, wr_ref[...], preferred_element_type=F32, precision=lax.Precision.HIGHEST) + br_ref[...]


def _router(x, mods, g2, wr, br, n_ctx_blocks, blk0):
    b, t, d = x.shape
    nblk = t // TOK
    tok = lambda w: pl.BlockSpec((1, TOK, w), lambda bi, i: (bi, i + blk0, 0))
    const = lambda a: pl.BlockSpec(a.shape, lambda bi, i: (0, 0))
    return pl.pallas_call(
        _router_kernel,
        out_shape=[jax.ShapeDtypeStruct(x.shape, BF16), jax.ShapeDtypeStruct((b, t, LANES), F32)],
        grid=(b, nblk - blk0),
        in_specs=[
            tok(d),
            pl.BlockSpec((1, 1, 8, d), lambda bi, i: (bi, jnp.where(i + blk0 >= n_ctx_blocks, 1, 0), 0, 0)),
            const(g2), const(wr), const(br),
        ],
        out_specs=[tok(d), tok(LANES)],
        compiler_params=_cparams("parallel", "parallel"),
        name="router",
    )(x, mods, g2, wr, br)


def _moe_kernel(be_ref, nv_ref, xs_ref, rw_ref, w1_ref, w3_ref, w2_ref, o_ref, acc_ref):
    i = pl.program_id(0)
    j = pl.program_id(1)
    nj = pl.num_programs(1)
    live = i < nv_ref[0]

    @pl.when(live)
    def _():
        xs = xs_ref[...]
        u = (_silu(_dot(xs, w1_ref[0])) * _dot(xs, w3_ref[0])).astype(BF16)
        part = _dot(u, w2_ref[0])

        @pl.when(j == 0)
        def _():
            acc_ref[...] = part

        @pl.when(j > 0)
        def _():
            acc_ref[...] += part

    @pl.when(j == nj - 1)
    def _():
        o_ref[...] = jnp.where(live, acc_ref[...] * rw_ref[...], 0.0)


def _moe_experts(block_expert, n_live, xs, row_w, w1, w3, w2):
    n_pad, d = xs.shape
    f = w1.shape[2]
    nj = f // MOE_TF
    n_blocks = n_pad // MOE_TM

    def jj(i, j, nv):
        return jnp.where(i < nv[0], j, nj - 1)

    grid_spec = pltpu.PrefetchScalarGridSpec(
        num_scalar_prefetch=2,
        grid=(n_blocks, nj),
        in_specs=[
            pl.BlockSpec((MOE_TM, d), lambda i, j, be, nv: (i, 0)),
            pl.BlockSpec((MOE_TM, 1), lambda i, j, be, nv: (i, 0)),
            pl.BlockSpec((1, d, MOE_TF), lambda i, j, be, nv: (be[i], 0, jj(i, j, nv))),
            pl.BlockSpec((1, d, MOE_TF), lambda i, j, be, nv: (be[i], 0, jj(i, j, nv))),
            pl.BlockSpec((1, MOE_TF, d), lambda i, j, be, nv: (be[i], jj(i, j, nv), 0)),
        ],
        out_specs=pl.BlockSpec((MOE_TM, d), lambda i, j, be, nv: (i, 0)),
        scratch_shapes=[pltpu.VMEM((MOE_TM, d), F32)],
    )
    return pl.pallas_call(
        _moe_kernel,
        out_shape=jax.ShapeDtypeStruct((n_pad, d), F32),
        grid_spec=grid_spec,
        compiler_params=_cparams("parallel", "arbitrary"),
        name="moe_experts",
    )(block_expert, n_live, xs, row_w, w1, w3, w2)


def _moe_route(logits, n_tok):
    top_v, top_i = lax.top_k(logits, TOP_K)
    wts = jax.nn.softmax(top_v, axis=-1)
    flat_e = top_i.reshape(-1)
    onehot = (flat_e[:, None] == jnp.arange(N_EXPERTS)[None, :]).astype(jnp.int32)
    rank = jnp.sum((jnp.cumsum(onehot, axis=0) - onehot) * onehot, axis=1)
    counts = jnp.sum(onehot, axis=0)
    padded = ((counts + MOE_TM - 1) // MOE_TM) * MOE_TM
    ends = jnp.cumsum(padded)
    offs = ends - padded
    dest = offs[flat_e] + rank
    n_pad = TOP_K * n_tok + N_EXPERTS * MOE_TM
    src_tok = jnp.zeros((n_pad,), jnp.int32).at[dest].set(jnp.arange(TOP_K * n_tok, dtype=jnp.int32) // TOP_K)
    row_w = jnp.zeros((n_pad,), F32).at[dest].set(wts.reshape(-1))
    blk_start = jnp.arange(n_pad // MOE_TM, dtype=jnp.int32) * MOE_TM
    block_expert = jnp.minimum(jnp.searchsorted(ends, blk_start, side="right"), N_EXPERTS - 1).astype(jnp.int32)
    n_live = (ends[-1] // MOE_TM).astype(jnp.int32).reshape(1)
    return src_tok, row_w.reshape(n_pad, 1), dest.reshape(n_tok, TOP_K), block_expert, n_live


def _final_kernel(x_ref, g_ref, o_ref):
    x = x_ref[0]
    ms = jnp.mean(x * x, axis=-1, keepdims=True)
    o_ref[0] = (x * lax.rsqrt(ms + EPS)) * g_ref[...]


def _final_norm(x, g, blk0, n_out):
    b, t, d = x.shape
    return pl.pallas_call(
        _final_kernel,
        out_shape=jax.ShapeDtypeStruct((b, n_out, d), F32),
        grid=(b, n_out // TOK),
        in_specs=[pl.BlockSpec((1, TOK, d), lambda bi, i: (bi, i + blk0, 0)),
                  pl.BlockSpec((1, d), lambda bi, i: (0, 0))],
        out_specs=pl.BlockSpec((1, TOK, d), lambda bi, i: (bi, i, 0)),
        compiler_params=_cparams("parallel", "parallel"),
        name="final_norm",
    )(x, g)


def _rope_tiles(n_ctx, seq):
    n_freq = HEAD_DIM // 4
    pos = jnp.arange(seq)
    row = (pos // GRID_W).astype(F32)
    col = (pos % GRID_W).astype(F32)
    inv = ROPE_BASE ** (-jnp.arange(n_freq, dtype=F32) / n_freq)
    ang = jnp.concatenate([row[:, None] * inv[None, :], col[:, None] * inv[None, :]], axis=-1)
    cos = jnp.concatenate([jnp.ones((n_ctx, 2 * n_freq), F32), jnp.cos(ang)], axis=0)
    sin = jnp.concatenate([jnp.zeros((n_ctx, 2 * n_freq), F32), jnp.sin(ang)], axis=0)
    return jnp.tile(cos, (1, 4)), jnp.tile(jnp.concatenate([-sin, sin], axis=1), (1, 2))


def kernel(x, c, ctx, c_ctx, w_ada, b_ada, norm1_g, norm2_g, final_g, w_in, b_gate, qn_a, kn_a, ret_a, ret_g,
           sink_c, w_br_a, w_br_b, w_br_c, w_out, ffn_w1, ffn_w3, ffn_w2, router_w, router_b, exp_w1, exp_w3,
           exp_w2):
    b, seq, d = x.shape
    n_ctx = ctx.shape[1]
    depth = w_ada.shape[0]
    t = n_ctx + seq
    assert n_ctx % TOK == 0 and seq % TOK == 0 and seq >= TOK + 2 * WINDOW and b + 1 <= 8
    ncb = n_ctx // TOK

    xa = jnp.concatenate([ctx, x], axis=1)
    c8 = jnp.zeros((8, d), F32).at[:b].set(c).at[b].set(c_ctx)
    mods_all = _ada_mods(c8, w_ada, b_ada).reshape(depth, 8, 6, d)
    cos, sin = _rope_tiles(n_ctx, seq)
    head_ids = jnp.arange(512) // HEAD_DIM
    pmat = jnp.where(head_ids[:, None] == head_ids[None, :], 1.0 / HEAD_DIM, 0.0).astype(BF16)
    lg_all = -jax.nn.softplus(-ret_a.astype(F32))

    for l in range(depth):
        need_ctx = l < depth - 1
        blk0 = 0 if need_ctx else ncb
        mod_l = mods_all[l, :b]
        mod_c = jnp.broadcast_to(mods_all[l, b][None], (b, 6, d))
        mods = jnp.pad(jnp.stack([mod_c, mod_l], axis=1), ((0, 0), (0, 0), (0, 2), (0, 0)))
        g1 = norm1_g[l].reshape(1, d)
        w_l = w_in[l].astype(BF16)
        qa, ka, va, qr, kr, vr, gr, qc, kc, vc = _proj(
            xa, mods, g1, w_l[:, :N_QKV], cos, sin, pmat,
            jnp.tile(qn_a[l], 2).reshape(1, LANES), jnp.tile(kn_a[l], 2).reshape(1, LANES), ncb)
        ya = _attn_a(qa, ka, va, n_ctx, blk0)
        yc = _attn_c(sink_c[l].astype(F32), qc, kc, vc, n_ctx, blk0)
        o_f, o_b = _retention(lg_all[l], qr, kr, vr, n_ctx)
        xa = _merge(xa, mods, g1, w_l[:, N_QKV:], b_gate[l].reshape(1, -1), ya, o_f, o_b, gr,
                    ret_g[l].reshape(1, -1), yc, w_br_a[l].astype(BF16), w_br_b[l].astype(BF16),
                    w_br_c[l].astype(BF16), w_out[l].astype(BF16), ncb, blk0)
        g2 = norm2_g[l].reshape(1, d)
        jx = l // 2
        if l % 2 == 0:
            xa = _ffn(xa, mods, g2, ffn_w1[jx].astype(BF16), ffn_w3[jx].astype(BF16), ffn_w2[jx].astype(BF16),
                      ncb, blk0)
        else:
            wr = jnp.zeros((d, LANES), F32).at[:, :N_EXPERTS].set(router_w[jx])
            br = jnp.zeros((1, LANES), F32).at[0, :N_EXPERTS].set(router_b[jx])
            h2, logits = _router(xa, mods, g2, wr, br, ncb, blk0)
            t0 = blk0 * TOK
            n_tok = b * (t - t0)
            h2 = h2[:, t0:].reshape(n_tok, d)
            src_tok, row_w, dest, block_expert, n_live = _moe_route(
                logits[:, t0:, :N_EXPERTS].reshape(n_tok, N_EXPERTS), n_tok)
            ys = _moe_experts(block_expert, n_live, h2[src_tok], row_w,
                              exp_w1[jx].astype(BF16), exp_w3[jx].astype(BF16), exp_w2[jx].astype(BF16))
            f = (ys[dest[:, 0]] + ys[dest[:, 1]]).reshape(b, t - t0, d)
            gate2 = mods[:, :, 5, :]
            gate_tok = jnp.concatenate([jnp.broadcast_to(gate2[:, 0:1], (b, n_ctx, d)),
                                        jnp.broadcast_to(gate2[:, 1:2], (b, seq, d))], axis=1)[:, t0:]
            xa = xa.at[:, t0:].add(gate_tok * f)
    return _final_norm(xa, final_g.reshape(1, d), ncb, seq)
```
